```python
import jax, jax.numpy as jnp
from jax import lax
import numpy as np

D_MODEL = 1024
BATCH = 8
SEQ = 4096
DEPTH = 1

D_MIX = D_MODEL
N_CONV = D_MIX // 2
CONV_HEADS = 8
N_POOL = D_MIX - N_CONV
POOL_WINDOWS = (2, 4, 8, 16)
POOL_GROUP = N_POOL // len(POOL_WINDOWS)
CONV_K = 31
D_FF = 2816
FFN_CONV_K = 3
N_MOD = 6
EPS = 1e-6

kernel_name = "hybrid_conformerconv_multipool_convffn_adaln"


def rmsnorm(x, g):
    xf = x.astype(jnp.float32)
    y = xf * lax.rsqrt(jnp.mean(xf * xf, axis=-1, keepdims=True) + EPS)
    return (y * g.astype(jnp.float32)).astype(x.dtype)


def layernorm(x, g, b):
    xf = x.astype(jnp.float32)
    mu = jnp.mean(xf, axis=-1, keepdims=True)
    xc = xf - mu
    var = jnp.mean(xc * xc, axis=-1, keepdims=True)
    y = xc * lax.rsqrt(var + EPS) * g.astype(jnp.float32) + b.astype(jnp.float32)
    return y.astype(x.dtype)


def modulate(h, shift, scale):
    return h * (1 + scale[:, None, :]) + shift[:, None, :]


def causal_depthwise_conv(u, w, b):
    k, ch = w.shape
    out = lax.conv_general_dilated(
        u, w[:, None, :].astype(u.dtype), window_strides=(1,),
        padding=[(k - 1, 0)], dimension_numbers=("NWC", "WIO", "NWC"),
        feature_group_count=ch)
    return out + b


def multiscale_causal_pool(p, pool_w, pool_scale):
    bsz, s, _ = p.shape
    pf = p.astype(jnp.float32)
    cs = jnp.cumsum(pf, axis=1)
    t1 = jnp.arange(1, s + 1)
    outs = []
    for gi, w in enumerate(POOL_WINDOWS):
        sl = slice(gi * POOL_GROUP, (gi + 1) * POOL_GROUP)
        csg = cs[..., sl]
        prev = jnp.pad(csg, ((0, 0), (w, 0), (0, 0)))[:, :s]
        cnt = jnp.minimum(t1, w).astype(jnp.float32)[:, None]
        outs.append((csg - prev) / cnt - pf[..., sl])
    z = jnp.concatenate(outs, axis=-1).astype(p.dtype)
    z = z.reshape(bsz, s, len(POOL_WINDOWS), POOL_GROUP)
    z = jnp.einsum("bsgc,gcd->bsgd", z, pool_w).reshape(bsz, s, N_POOL)
    return z * pool_scale


def setup_inputs(seed: int = 0) -> dict:
    key = jax.random.key(seed)
    ks = jax.random.split(key, 20)
    f32 = jnp.float32
    nrm = lambda k, shp, fan: jax.random.normal(k, shp, f32) * (fan ** -0.5)
    gain = lambda k, n: jnp.ones((n,), f32) + 0.02 * jax.random.normal(k, (n,), f32)
    return {
        "x": jax.random.normal(ks[0], (BATCH, SEQ, D_MODEL), f32),
        "c": jax.random.normal(ks[1], (BATCH, D_MODEL), f32),
        "w_ada": nrm(ks[2], (D_MODEL, N_MOD * D_MODEL), D_MODEL),
        "b_ada": 0.02 * jax.random.normal(ks[3], (N_MOD * D_MODEL,), f32),
        "g_mix": gain(ks[4], D_MODEL),
        "w_in": nrm(ks[5], (D_MODEL, 2 * N_CONV + N_POOL), D_MODEL),
        "conv_w": nrm(ks[6], (CONV_K, N_CONV), CONV_K),
        "conv_b": 0.02 * jax.random.normal(ks[7], (N_CONV,), f32),
        "ln_g": gain(ks[8], N_CONV),
        "ln_b": 0.02 * jax.random.normal(ks[9], (N_CONV,), f32),
        "pool_w": nrm(ks[10], (len(POOL_WINDOWS), POOL_GROUP, POOL_GROUP), POOL_GROUP),
        "pool_scale": gain(ks[11], N_POOL),
        "w_out": nrm(ks[12], (D_MIX, D_MODEL), D_MIX),
        "g_ffn": gain(ks[13], D_MODEL),
        "w_up": nrm(ks[14], (D_MODEL, 2 * D_FF), D_MODEL),
        "ffn_conv_w": nrm(ks[15], (FFN_CONV_K, 2 * D_FF), FFN_CONV_K),
        "ffn_conv_b": 0.02 * jax.random.normal(ks[16], (2 * D_FF,), f32),
        "w_down": nrm(ks[17], (D_FF, D_MODEL), D_FF),
        "g_final": gain(ks[18], D_MODEL),
    }


def reference(x, c, w_ada, b_ada, g_mix, w_in, conv_w, conv_b, ln_g, ln_b,
              pool_w, pool_scale, w_out, g_ffn, w_up, ffn_conv_w, ffn_conv_b,
              w_down, g_final):
    mod = jax.nn.silu(c) @ w_ada + b_ada
    sh1, sc1, gt1, sh2, sc2, gt2 = jnp.split(mod, N_MOD, axis=-1)

    for _ in range(DEPTH):
        h = modulate(rmsnorm(x, g_mix), sh1, sc1)
        proj = h @ w_in
        a_val = proj[..., :N_CONV]
        a_gate = proj[..., N_CONV:2 * N_CONV]
        p = proj[..., 2 * N_CONV:]

        u = a_val * jax.nn.sigmoid(a_gate)
        u = causal_depthwise_conv(u, conv_w, conv_b)
        ya = jax.nn.silu(layernorm(u, ln_g, ln_b))

        yb = multiscale_causal_pool(p, pool_w, pool_scale)

        y = jnp.concatenate([ya, yb], axis=-1) @ w_out
        x = x + gt1[:, None, :] * y

        h = modulate(rmsnorm(x, g_ffn), sh2, sc2)
        up = causal_depthwise_conv(h @ w_up, ffn_conv_w, ffn_conv_b)
        v = up[..., :D_FF]
        g = up[..., D_FF:]
        x = x + gt2[:, None, :] * ((jax.nn.silu(g) * v) @ w_down)

    return rmsnorm(x, g_final)
```

```python
import functools

import jax
import jax.numpy as jnp
from jax import lax
from jax.experimental import pallas as pl
from jax.experimental.pallas import tpu as pltpu

EPS = 1e-6
POOL_WINDOWS = (2, 4, 8, 16)

LANES = 128
SUBLANES = 8
BF16_ROWS = 16
MXU_DIM = 256
VMEM_LIMIT_BYTES = 56 * 1024 * 1024

SEQ_TILE = 512
ROW_BLOCK = 64
CONV_HALO = 32
POOL_HALO = 16
FFN_HALO = SUBLANES
FFN_CHUNK = MXU_DIM

f32 = jnp.float32
bf16 = jnp.bfloat16


def _split_bf16(a):
    hi = a.astype(bf16)
    lo = (a - hi.astype(f32)).astype(bf16)
    return hi, lo


def _adaln_kernel(c_ref, w_ref, b_ref, o_ref):
    c = c_ref[...]
    s = c * jax.nn.sigmoid(c)
    s_hi, s_lo = _split_bf16(s)
    w_hi, w_lo = _split_bf16(w_ref[...])
    acc = jnp.dot(s_hi, w_hi, preferred_element_type=f32)
    acc += jnp.dot(s_lo, w_hi, preferred_element_type=f32)
    acc += jnp.dot(s_hi, w_lo, preferred_element_type=f32)
    o_ref[...] = acc + b_ref[...]


def _adaln(c, w_ada, b_ada):
    bsz, d = c.shape
    n = w_ada.shape[1]
    return pl.pallas_call(
        _adaln_kernel,
        grid=(n // d,),
        in_specs=[
            pl.BlockSpec((bsz, d), lambda j: (0, 0)),
            pl.BlockSpec((d, d), lambda j: (0, j)),
            pl.BlockSpec((1, d), lambda j: (0, j)),
        ],
        out_specs=pl.BlockSpec((bsz, d), lambda j: (0, j)),
        out_shape=jax.ShapeDtypeStruct((bsz, n), f32),
        compiler_params=pltpu.CompilerParams(dimension_semantics=("arbitrary",)),
        name="adaln",
    )(c, w_ada, b_ada.reshape(1, n))


def _mixer_kernel(x_ref, g_ref, sc_ref, sh_ref, gt_ref, w_in_ref, cw_ref, cb_ref, lng_ref,
                  lnb_ref, pw_ref, ps_ref, w_out_ref, o_ref,
                  proj_ref, ubuf, pbuf, zbuf, ybuf, *, n_conv, conv_k):
    ts = x_ref.shape[1]
    n_slab = n_conv // LANES
    n_grp = len(POOL_WINDOWS)
    i = pl.program_id(1)

    @pl.when(i == 0)
    def _():
        ubuf[:, 0:CONV_HALO, :] = jnp.zeros((n_slab, CONV_HALO, LANES), f32)
        pbuf[:, 0:POOL_HALO, :] = jnp.zeros((n_grp, POOL_HALO, LANES), f32)

    x = x_ref[0]
    ms = jnp.mean(x * x, axis=-1, keepdims=True)
    h = (x * lax.rsqrt(ms + EPS)) * (g_ref[...] * (1.0 + sc_ref[0])) + sh_ref[0]
    proj_ref[...] = jnp.dot(h.astype(bf16), w_in_ref[...], preferred_element_type=f32)

    for rb in range(0, ts, ROW_BLOCK):
        rows = pl.ds(rb, ROW_BLOCK)
        for j in range(n_slab):
            a_val = proj_ref[rows, j * LANES:(j + 1) * LANES]
            a_gate = proj_ref[rows, n_conv + j * LANES:n_conv + (j + 1) * LANES]
            ubuf[j, pl.ds(CONV_HALO + rb, ROW_BLOCK), :] = a_val * jax.nn.sigmoid(a_gate)
        for j in range(n_grp):
            pbuf[j, pl.ds(POOL_HALO + rb, ROW_BLOCK), :] = (
                proj_ref[rows, 2 * n_conv + j * LANES:2 * n_conv + (j + 1) * LANES])

    t0 = (i * ts).astype(f32)
    for rb in range(0, ts, ROW_BLOCK):
        conv = []
        for j in range(n_slab):
            lanes = slice(j * LANES, (j + 1) * LANES)
            acc = jnp.broadcast_to(cb_ref[:, lanes], (ROW_BLOCK, LANES))
            for k in range(conv_k):
                off = CONV_HALO + rb - (conv_k - 1) + k
                acc = acc + cw_ref[k:k + 1, lanes] * ubuf[j, pl.ds(off, ROW_BLOCK), :]
            conv.append(acc)
        tot = conv[0]
        for j in range(1, n_slab):
            tot = tot + conv[j]
        mu = jnp.sum(tot, axis=-1, keepdims=True) * (1.0 / n_conv)
        cen = [cj - mu for cj in conv]
        sq = cen[0] * cen[0]
        for j in range(1, n_slab):
            sq = sq + cen[j] * cen[j]
        var = jnp.sum(sq, axis=-1, keepdims=True) * (1.0 / n_conv)
        rstd = lax.rsqrt(var + EPS)
        for j in range(n_slab):
            lanes = slice(j * LANES, (j + 1) * LANES)
            yn = cen[j] * rstd * lng_ref[:, lanes] + lnb_ref[:, lanes]
            ybuf[pl.ds(rb, ROW_BLOCK), lanes] = (yn * jax.nn.sigmoid(yn)).astype(bf16)

        t = t0 + (rb + lax.broadcasted_iota(jnp.int32, (ROW_BLOCK, LANES), 0)).astype(f32)
        for j, win in enumerate(POOL_WINDOWS):
            cur = pbuf[j, pl.ds(POOL_HALO + rb, ROW_BLOCK), :]
            s = cur
            for k in range(1, win):
                s = s + pbuf[j, pl.ds(POOL_HALO + rb - k, ROW_BLOCK), :]
            cnt = jnp.minimum(t + 1.0, float(win))
            zbuf[pl.ds(rb, ROW_BLOCK), j * LANES:(j + 1) * LANES] = (s / cnt - cur).astype(bf16)

    ubuf[:, 0:CONV_HALO, :] = ubuf[:, ts:ts + CONV_HALO, :]
    pbuf[:, 0:POOL_HALO, :] = pbuf[:, ts:ts + POOL_HALO, :]

    for q in range(pw_ref.shape[0]):
        cols = slice(q * MXU_DIM, (q + 1) * MXU_DIM)
        yb = jnp.dot(zbuf[:, cols], pw_ref[q], preferred_element_type=f32) * ps_ref[:, cols]
        ybuf[:, n_conv + q * MXU_DIM:n_conv + (q + 1) * MXU_DIM] = yb.astype(bf16)

    y = jnp.dot(ybuf[...], w_out_ref[...], preferred_element_type=f32)
    o_ref[0] = x_ref[0] + gt_ref[0] * y


def _resident(shape):
    zeros = (0,) * len(shape)
    return pl.BlockSpec(shape, lambda b, i: zeros, pipeline_mode=pl.Buffered(1))


def _per_batch(d):
    return pl.BlockSpec((1, 1, d), lambda b, i: (b, 0, 0))


def _mixer(x, g_mix, sc1, sh1, gt1, w_in, conv_w, conv_b, ln_g, ln_b, pool_bd, pool_scale, w_out):
    bsz, seq, d = x.shape
    conv_k, n_conv = conv_w.shape
    n_pool = pool_scale.shape[0]
    ts = SEQ_TILE
    assert seq % ts == 0 and ts % ROW_BLOCK == 0 and ROW_BLOCK % BF16_ROWS == 0
    assert n_conv % LANES == 0 and n_pool == len(POOL_WINDOWS) * LANES
    assert CONV_HALO >= conv_k - 1 and POOL_HALO >= max(POOL_WINDOWS) - 1
    kern = functools.partial(_mixer_kernel, n_conv=n_conv, conv_k=conv_k)
    tile = pl.BlockSpec((1, ts, d), lambda b, i: (b, i, 0))
    return pl.pallas_call(
        kern,
        grid=(bsz, seq // ts),
        in_specs=[
            tile, _resident((1, d)), _per_batch(d), _per_batch(d), _per_batch(d),
            _resident(w_in.shape), _resident(conv_w.shape), _resident((1, n_conv)),
            _resident((1, n_conv)), _resident((1, n_conv)), _resident(pool_bd.shape),
            _resident((1, n_pool)), _resident(w_out.shape),
        ],
        out_specs=tile,
        out_shape=jax.ShapeDtypeStruct(x.shape, f32),
        scratch_shapes=[
            pltpu.VMEM((ts, w_in.shape[1]), f32),
            pltpu.VMEM((n_conv // LANES, CONV_HALO + ts, LANES), f32),
            pltpu.VMEM((len(POOL_WINDOWS), POOL_HALO + ts, LANES), f32),
            pltpu.VMEM((ts, n_pool), bf16),
            pltpu.VMEM((ts, n_conv + n_pool), bf16),
        ],
        compiler_params=pltpu.CompilerParams(
            dimension_semantics=("arbitrary", "arbitrary"),
            vmem_limit_bytes=VMEM_LIMIT_BYTES),
        name="mixer",
    )(x, g_mix.reshape(1, d), sc1, sh1, gt1, w_in, conv_w, conv_b.reshape(1, n_conv),
      ln_g.reshape(1, n_conv), ln_b.reshape(1, n_conv), pool_bd, pool_scale.reshape(1, n_pool), w_out)


def _ffn_kernel(x_ref, g_ref, sc_ref, sh_ref, gt_ref, w_up_ref, fw_ref, fb_ref, w_down_ref,
                gf_ref, o_ref, hbuf, upbuf, carry, abuf, *, d_ff):
    ts = x_ref.shape[1]
    n_chunk = d_ff // FFN_CHUNK
    slabs = FFN_CHUNK // LANES
    i = pl.program_id(1)

    @pl.when(i == 0)
    def _():
        carry[...] = jnp.zeros(carry.shape, f32)

    x = x_ref[0]
    ms = jnp.mean(x * x, axis=-1, keepdims=True)
    h = (x * lax.rsqrt(ms + EPS)) * (g_ref[...] * (1.0 + sc_ref[0])) + sh_ref[0]
    hbuf[...] = h.astype(bf16)

    def conv3(buf, slab, col, rb):
        lanes = slice(col, col + LANES)
        acc = jnp.broadcast_to(fb_ref[:, lanes], (ROW_BLOCK, LANES))
        for k in range(3):
            acc = acc + fw_ref[k:k + 1, lanes] * buf[slab, pl.ds(FFN_HALO + rb - 2 + k, ROW_BLOCK), :]
        return acc

    for c in range(n_chunk):
        buf = upbuf.at[c % 2]
        cols = [c * FFN_CHUNK + s * LANES for s in range(slabs)]
        cols = cols + [d_ff + col for col in cols]
        for half in range(2):
            c0 = half * d_ff + c * FFN_CHUNK
            up = jnp.dot(hbuf[...], w_up_ref[:, c0:c0 + FFN_CHUNK], preferred_element_type=f32)
            for s in range(slabs):
                slab = half * slabs + s
                buf[slab, 0:FFN_HALO, :] = carry[c * 2 * slabs + slab]
                buf[slab, FFN_HALO:FFN_HALO + ts, :] = up[:, s * LANES:(s + 1) * LANES]
                carry[c * 2 * slabs + slab] = buf[slab, ts:ts + FFN_HALO, :]
        for rb in range(0, ts, ROW_BLOCK):
            for s in range(slabs):
                v = conv3(buf, s, cols[s], rb)
                g = conv3(buf, slabs + s, cols[slabs + s], rb)
                act = (g * jax.nn.sigmoid(g)) * v
                abuf[pl.ds(rb, ROW_BLOCK), cols[s]:cols[s] + LANES] = act.astype(bf16)

    y = jnp.dot(abuf[...], w_down_ref[...], preferred_element_type=f32)
    x2 = x_ref[0] + gt_ref[0] * y
    ms2 = jnp.mean(x2 * x2, axis=-1, keepdims=True)
    o_ref[0] = (x2 * lax.rsqrt(ms2 + EPS)) * gf_ref[...]


def _ffn(x, g_ffn, sc2, sh2, gt2, w_up, ffn_conv_w, ffn_conv_b, w_down, g_final):
    bsz, seq, d = x.shape
    d_ff = w_down.shape[0]
    ts = SEQ_TILE
    assert seq % ts == 0 and d_ff % FFN_CHUNK == 0 and FFN_HALO >= ffn_conv_w.shape[0] - 1
    kern = functools.partial(_ffn_kernel, d_ff=d_ff)
    tile = pl.BlockSpec((1, ts, d), lambda b, i: (b, i, 0))
    n_slab = 2 * d_ff // LANES
    return pl.pallas_call(
        kern,
        grid=(bsz, seq // ts),
        in_specs=[
            tile, _resident((1, d)), _per_batch(d), _per_batch(d), _per_batch(d),
            _resident(w_up.shape), _resident(ffn_conv_w.shape), _resident((1, 2 * d_ff)),
            _resident(w_down.shape), _resident((1, d)),
        ],
        out_specs=tile,
        out_shape=jax.ShapeDtypeStruct(x.shape, f32),
        scratch_shapes=[
            pltpu.VMEM((ts, d), bf16),
            pltpu.VMEM((2, 2 * FFN_CHUNK // LANES, FFN_HALO + ts, LANES), f32),
            pltpu.VMEM((n_slab, FFN_HALO, LANES), f32),
            pltpu.VMEM((ts, d_ff), bf16),
        ],
        compiler_params=pltpu.CompilerParams(
            dimension_semantics=("arbitrary", "arbitrary"),
            vmem_limit_bytes=VMEM_LIMIT_BYTES),
        name="convffn",
    )(x, g_ffn.reshape(1, d), sc2, sh2, gt2, w_up, ffn_conv_w, ffn_conv_b.reshape(1, 2 * d_ff),
      w_down, g_final.reshape(1, d))


def kernel(x, c, w_ada, b_ada, g_mix, w_in, conv_w, conv_b, ln_g, ln_b, pool_w, pool_scale, w_out,
           g_ffn, w_up, ffn_conv_w, ffn_conv_b, w_down, g_final):
    bsz, seq, d = x.shape
    mod = _adaln(c, w_ada, b_ada)
    sh1, sc1, gt1, sh2, sc2, gt2 = [m.reshape(bsz, 1, d) for m in jnp.split(mod, 6, axis=-1)]

    n_grp, grp, _ = pool_w.shape
    per_tile = MXU_DIM // grp
    pool_bd = jnp.zeros((n_grp // per_tile, MXU_DIM, MXU_DIM), pool_w.dtype)
    for gi in range(n_grp):
        q, r = divmod(gi, per_tile)
        pool_bd = pool_bd.at[q, r * grp:(r + 1) * grp, r * grp:(r + 1) * grp].set(pool_w[gi])

    x1 = _mixer(x, g_mix, sc1, sh1, gt1, w_in.astype(bf16), conv_w, conv_b, ln_g, ln_b,
                pool_bd.astype(bf16), pool_scale, w_out.astype(bf16))
    return _ffn(x1, g_ffn, sc2, sh2, gt2, w_up.astype(bf16), ffn_conv_w, ffn_conv_b,
                w_down.astype(bf16), g_final)
```

```python
import functools

import jax
import jax.numpy as jnp
from jax import lax
from jax.experimental import pallas as pl
from jax.experimental.pallas import tpu as pltpu

EPS = 1e-6
POOL_WINDOWS = (2, 4, 8, 16)

LANES = 128
SUBLANES = 8
BF16_ROWS = 16
MXU_DIM = 256
VMEM_LIMIT_BYTES = 58 * 1024 * 1024

MIXER_TILE = 512
FFN_TILE = 1024
ROW_BLOCK = 64
CONV_HALO = 32
POOL_HALO = 16
FFN_HALO = SUBLANES
FFN_CHUNK = MXU_DIM
ACT_PARTS = 3

f32 = jnp.float32
bf16 = jnp.bfloat16


def _split_bf16(a):
    hi = a.astype(bf16)
    lo = (a - hi.astype(f32)).astype(bf16)
    return hi, lo


def _adaln_kernel(c_ref, w_ref, b_ref, o_ref):
    c = c_ref[...]
    s = c * jax.nn.sigmoid(c)
    s_hi, s_lo = _split_bf16(s)
    w_hi, w_lo = _split_bf16(w_ref[...])
    acc = jnp.dot(s_hi, w_hi, preferred_element_type=f32)
    acc += jnp.dot(s_lo, w_hi, preferred_element_type=f32)
    acc += jnp.dot(s_hi, w_lo, preferred_element_type=f32)
    o_ref[...] = acc + b_ref[...]


def _adaln(c, w_ada, b_ada):
    bsz, d = c.shape
    n = w_ada.shape[1]
    return pl.pallas_call(
        _adaln_kernel,
        grid=(n // d,),
        in_specs=[
            pl.BlockSpec((bsz, d), lambda j: (0, 0)),
            pl.BlockSpec((d, d), lambda j: (0, j)),
            pl.BlockSpec((1, d), lambda j: (0, j)),
        ],
        out_specs=pl.BlockSpec((bsz, d), lambda j: (0, j)),
        out_shape=jax.ShapeDtypeStruct((bsz, n), f32),
        compiler_params=pltpu.CompilerParams(dimension_semantics=("arbitrary",)),
        name="adaln",
    )(c, w_ada, b_ada.reshape(1, n))


def _mixer_kernel(x_ref, g_ref, sc_ref, sh_ref, gt_ref, w_in_ref, cw_ref, cb_ref, lng_ref,
                  lnb_ref, pw_ref, ps_ref, w_out_ref, o_ref,
                  proj_ref, ubuf, pbuf, zbuf, ybuf, *, n_conv, conv_k):
    ts = x_ref.shape[1]
    n_slab = n_conv // LANES
    n_grp = len(POOL_WINDOWS)
    i = pl.program_id(1)

    @pl.when(i == 0)
    def _():
        ubuf[:, 0:CONV_HALO, :] = jnp.zeros((n_slab, CONV_HALO, LANES), f32)
        pbuf[:, 0:POOL_HALO, :] = jnp.zeros((n_grp, POOL_HALO, LANES), f32)

    x = x_ref[0]
    ms = jnp.mean(x * x, axis=-1, keepdims=True)
    h = (x * lax.rsqrt(ms + EPS)) * (g_ref[...] * (1.0 + sc_ref[0])) + sh_ref[0]
    proj_ref[...] = jnp.dot(h.astype(bf16), w_in_ref[...], preferred_element_type=f32)

    for rb in range(0, ts, ROW_BLOCK):
        rows = pl.ds(rb, ROW_BLOCK)
        for j in range(n_slab):
            a_val = proj_ref[rows, j * LANES:(j + 1) * LANES]
            a_gate = proj_ref[rows, n_conv + j * LANES:n_conv + (j + 1) * LANES]
            ubuf[j, pl.ds(CONV_HALO + rb, ROW_BLOCK), :] = a_val * jax.nn.sigmoid(a_gate)
        for j in range(n_grp):
            pbuf[j, pl.ds(POOL_HALO + rb, ROW_BLOCK), :] = (
                proj_ref[rows, 2 * n_conv + j * LANES:2 * n_conv + (j + 1) * LANES])

    t0 = (i * ts).astype(f32)
    for rb in range(0, ts, ROW_BLOCK):
        conv = []
        for j in range(n_slab):
            lanes = slice(j * LANES, (j + 1) * LANES)
            acc = jnp.broadcast_to(cb_ref[:, lanes], (ROW_BLOCK, LANES))
            for k in range(conv_k):
                off = CONV_HALO + rb - (conv_k - 1) + k
                acc = acc + cw_ref[k:k + 1, lanes] * ubuf[j, pl.ds(off, ROW_BLOCK), :]
            conv.append(acc)
        tot = conv[0]
        for j in range(1, n_slab):
            tot = tot + conv[j]
        mu = jnp.sum(tot, axis=-1, keepdims=True) * (1.0 / n_conv)
        cen = [cj - mu for cj in conv]
        sq = cen[0] * cen[0]
        for j in range(1, n_slab):
            sq = sq + cen[j] * cen[j]
        var = jnp.sum(sq, axis=-1, keepdims=True) * (1.0 / n_conv)
        rstd = lax.rsqrt(var + EPS)
        for j in range(n_slab):
            lanes = slice(j * LANES, (j + 1) * LANES)
            yn = cen[j] * rstd * lng_ref[:, lanes] + lnb_ref[:, lanes]
            ybuf[pl.ds(rb, ROW_BLOCK), lanes] = (yn * jax.nn.sigmoid(yn)).astype(bf16)

        t = t0 + (rb + lax.broadcasted_iota(jnp.int32, (ROW_BLOCK, LANES), 0)).astype(f32)
        for j, win in enumerate(POOL_WINDOWS):
            cur = pbuf[j, pl.ds(POOL_HALO + rb, ROW_BLOCK), :]
            s = cur
            for k in range(1, win):
                s = s + pbuf[j, pl.ds(POOL_HALO + rb - k, ROW_BLOCK), :]
            cnt = jnp.minimum(t + 1.0, float(win))
            zbuf[pl.ds(rb, ROW_BLOCK), j * LANES:(j + 1) * LANES] = (s / cnt - cur).astype(bf16)

    ubuf[:, 0:CONV_HALO, :] = ubuf[:, ts:ts + CONV_HALO, :]
    pbuf[:, 0:POOL_HALO, :] = pbuf[:, ts:ts + POOL_HALO, :]

    for q in range(pw_ref.shape[0]):
        cols = slice(q * MXU_DIM, (q + 1) * MXU_DIM)
        yb = jnp.dot(zbuf[:, cols], pw_ref[q], preferred_element_type=f32) * ps_ref[:, cols]
        ybuf[:, n_conv + q * MXU_DIM:n_conv + (q + 1) * MXU_DIM] = yb.astype(bf16)

    y = jnp.dot(ybuf[...], w_out_ref[...], preferred_element_type=f32)
    o_ref[0] = x_ref[0] + gt_ref[0] * y


def _resident(shape):
    zeros = (0,) * len(shape)
    return pl.BlockSpec(shape, lambda b, i: zeros, pipeline_mode=pl.Buffered(1))


def _per_batch(d):
    return pl.BlockSpec((1, 1, d), lambda b, i: (b, 0, 0))


def _mixer(x, g_mix, sc1, sh1, gt1, w_in, conv_w, conv_b, ln_g, ln_b, pool_bd, pool_scale, w_out):
    bsz, seq, d = x.shape
    conv_k, n_conv = conv_w.shape
    n_pool = pool_scale.shape[0]
    ts = MIXER_TILE
    assert seq % ts == 0 and ts % ROW_BLOCK == 0 and ROW_BLOCK % BF16_ROWS == 0
    assert n_conv % LANES == 0 and n_pool == len(POOL_WINDOWS) * LANES
    assert CONV_HALO >= conv_k - 1 and POOL_HALO >= max(POOL_WINDOWS) - 1
    kern = functools.partial(_mixer_kernel, n_conv=n_conv, conv_k=conv_k)
    tile = pl.BlockSpec((1, ts, d), lambda b, i: (b, i, 0))
    return pl.pallas_call(
        kern,
        grid=(bsz, seq // ts),
        in_specs=[
            tile, _resident((1, d)), _per_batch(d), _per_batch(d), _per_batch(d),
            _resident(w_in.shape), _resident(conv_w.shape), _resident((1, n_conv)),
            _resident((1, n_conv)), _resident((1, n_conv)), _resident(pool_bd.shape),
            _resident((1, n_pool)), _resident(w_out.shape),
        ],
        out_specs=tile,
        out_shape=jax.ShapeDtypeStruct(x.shape, f32),
        scratch_shapes=[
            pltpu.VMEM((ts, w_in.shape[1]), f32),
            pltpu.VMEM((n_conv // LANES, CONV_HALO + ts, LANES), f32),
            pltpu.VMEM((len(POOL_WINDOWS), POOL_HALO + ts, LANES), f32),
            pltpu.VMEM((ts, n_pool), bf16),
            pltpu.VMEM((ts, n_conv + n_pool), bf16),
        ],
        compiler_params=pltpu.CompilerParams(
            dimension_semantics=("arbitrary", "arbitrary"),
            vmem_limit_bytes=VMEM_LIMIT_BYTES),
        name="mixer",
    )(x, g_mix.reshape(1, d), sc1, sh1, gt1, w_in, conv_w, conv_b.reshape(1, n_conv),
      ln_g.reshape(1, n_conv), ln_b.reshape(1, n_conv), pool_bd, pool_scale.reshape(1, n_pool), w_out)


def _ffn_kernel(x_ref, g_ref, sc_ref, sh_ref, gt_ref, w_up_ref, fw_ref, fb_ref, w_down_ref,
                gf_ref, o_ref, hbuf, upbuf, carry, *abufs, d_ff):
    ts = x_ref.shape[1]
    n_chunk = d_ff // FFN_CHUNK
    part_chunks = pl.cdiv(n_chunk, len(abufs))
    slabs = FFN_CHUNK // LANES
    i = pl.program_id(1)

    @pl.when(i == 0)
    def _():
        carry[...] = jnp.zeros(carry.shape, f32)

    a_wr = [a.at[jnp.minimum(i, 0)] for a in abufs]
    a_rd = [a.at[jnp.minimum(pl.program_id(0), 0)] for a in abufs]

    x = x_ref[0]
    ms = jnp.mean(x * x, axis=-1, keepdims=True)
    h = (x * lax.rsqrt(ms + EPS)) * (g_ref[...] * (1.0 + sc_ref[0])) + sh_ref[0]
    hbuf[...] = h.astype(bf16)

    def conv3(buf, slab, col, rb):
        lanes = slice(col, col + LANES)
        acc = jnp.broadcast_to(fb_ref[:, lanes], (ROW_BLOCK, LANES))
        for k in range(3):
            acc = acc + fw_ref[k:k + 1, lanes] * buf[slab, pl.ds(FFN_HALO + rb - 2 + k, ROW_BLOCK), :]
        return acc

    for c in range(n_chunk):
        buf = upbuf.at[c % 2]
        cols = [c * FFN_CHUNK + s * LANES for s in range(slabs)]
        cols = cols + [d_ff + col for col in cols]
        for half in range(2):
            c0 = half * d_ff + c * FFN_CHUNK
            up = jnp.dot(hbuf[...], w_up_ref[:, c0:c0 + FFN_CHUNK], preferred_element_type=f32)
            for s in range(slabs):
                slab = half * slabs + s
                buf[slab, 0:FFN_HALO, :] = carry[c * 2 * slabs + slab]
                buf[slab, FFN_HALO:FFN_HALO + ts, :] = up[:, s * LANES:(s + 1) * LANES]
                carry[c * 2 * slabs + slab] = buf[slab, ts:ts + FFN_HALO, :]
        for rb in range(0, ts, ROW_BLOCK):
            for s in range(slabs):
                v = conv3(buf, s, cols[s], rb)
                g = conv3(buf, slabs + s, cols[slabs + s], rb)
                act = (g * jax.nn.sigmoid(g)) * v
                col = (c % part_chunks) * FFN_CHUNK + s * LANES
                a_wr[c // part_chunks][pl.ds(rb, ROW_BLOCK), col:col + LANES] = act.astype(bf16)

    y = None
    for p, part in enumerate(a_rd):
        k0 = p * part_chunks * FFN_CHUNK
        yp = jnp.dot(part[...], w_down_ref[k0:k0 + part.shape[1], :], preferred_element_type=f32)
        y = yp if y is None else y + yp
    x2 = x_ref[0] + gt_ref[0] * y
    ms2 = jnp.mean(x2 * x2, axis=-1, keepdims=True)
    o_ref[0] = (x2 * lax.rsqrt(ms2 + EPS)) * gf_ref[...]


def _split_even(items, n_parts):
    size = -(-len(items) // n_parts)
    return [items[k:k + size] for k in range(0, len(items), size)]


def _ffn(x, g_ffn, sc2, sh2, gt2, w_up, ffn_conv_w, ffn_conv_b, w_down, g_final):
    bsz, seq, d = x.shape
    d_ff = w_down.shape[0]
    ts = FFN_TILE
    assert seq % ts == 0 and ts % ROW_BLOCK == 0 and ROW_BLOCK % BF16_ROWS == 0
    assert d_ff % FFN_CHUNK == 0 and FFN_HALO >= ffn_conv_w.shape[0] - 1
    kern = functools.partial(_ffn_kernel, d_ff=d_ff)
    tile = pl.BlockSpec((1, ts, d), lambda b, i: (b, i, 0))
    n_slab = 2 * d_ff // LANES
    return pl.pallas_call(
        kern,
        grid=(bsz, seq // ts),
        in_specs=[
            tile, _resident((1, d)), _per_batch(d), _per_batch(d), _per_batch(d),
            _resident(w_up.shape), _resident(ffn_conv_w.shape), _resident((1, 2 * d_ff)),
            _resident(w_down.shape), _resident((1, d)),
        ],
        out_specs=tile,
        out_shape=jax.ShapeDtypeStruct(x.shape, f32),
        scratch_shapes=[
            pltpu.VMEM((ts, d), bf16),
            pltpu.VMEM((2, 2 * FFN_CHUNK // LANES, FFN_HALO + ts, LANES), f32),
            pltpu.VMEM((n_slab, FFN_HALO, LANES), f32),
        ] + [
            pltpu.VMEM((1, ts, FFN_CHUNK * len(chunks)), bf16)
            for chunks in _split_even(range(d_ff // FFN_CHUNK), ACT_PARTS)
        ],
        compiler_params=pltpu.CompilerParams(
            dimension_semantics=("arbitrary", "arbitrary"),
            vmem_limit_bytes=VMEM_LIMIT_BYTES),
        name="convffn",
    )(x, g_ffn.reshape(1, d), sc2, sh2, gt2, w_up, ffn_conv_w, ffn_conv_b.reshape(1, 2 * d_ff),
      w_down, g_final.reshape(1, d))


def kernel(x, c, w_ada, b_ada, g_mix, w_in, conv_w, conv_b, ln_g, ln_b, pool_w, pool_scale, w_out,
           g_ffn, w_up, ffn_conv_w, ffn_conv_b, w_down, g_final):
    bsz, seq, d = x.shape
    mod = _adaln(c, w_ada, b_ada)
    sh1, sc1, gt1, sh2, sc2, gt2 = [m.reshape(bsz, 1, d) for m in jnp.split(mod, 6, axis=-1)]

    n_grp, grp, _ = pool_w.shape
    per_tile = MXU_DIM // grp
    pool_bd = jnp.zeros((n_grp // per_tile, MXU_DIM, MXU_DIM), pool_w.dtype)
    for gi in range(n_grp):
        q, r = divmod(gi, per_tile)
        pool_bd = pool_bd.at[q, r * grp:(r + 1) * grp, r * grp:(r + 1) * grp].set(pool_w[gi])

    x1 = _mixer(x, g_mix, sc1, sh1, gt1, w_in.astype(bf16), conv_w, conv_b, ln_g, ln_b,
                pool_bd.astype(bf16), pool_scale, w_out.astype(bf16))
    return _ffn(x1, g_ffn, sc2, sh2, gt2, w_up.astype(bf16), ffn_conv_w, ffn_conv_b,
                w_down.astype(bf16), g_final)
```

```python
import functools

import jax
import jax.numpy as jnp
from jax import lax
from jax.experimental import pallas as pl
from jax.experimental.pallas import tpu as pltpu

EPS = 1e-6
POOL_WINDOWS = (2, 4, 8, 16)

LANES = 128
SUBLANES = 8
BF16_ROWS = 16
MXU_DIM = 256
VMEM_LIMIT_BYTES = 58 * 1024 * 1024

ADALN_BLOCK = 1024
MIXER_TILE = 1024
FFN_TILE = 1024
ROW_BLOCK = 64
FFN_ROW_BLOCK = 64
CONV_HALO = 32
POOL_HALO = 16
FFN_HALO = SUBLANES
FFN_CHUNK = MXU_DIM
ACT_PARTS = 11

f32 = jnp.float32
bf16 = jnp.bfloat16


def _split_bf16(a):
    hi = a.astype(bf16)
    lo = (a - hi.astype(f32)).astype(bf16)
    return hi, lo


def _adaln_kernel(c_ref, w_ref, b_ref, o_ref):
    c = c_ref[...]
    s = c * jax.nn.sigmoid(c)
    s_hi, s_lo = _split_bf16(s)
    w_hi, w_lo = _split_bf16(w_ref[...])
    acc = jnp.dot(s_hi, w_hi, preferred_element_type=f32)
    acc += jnp.dot(s_lo, w_hi, preferred_element_type=f32)
    acc += jnp.dot(s_hi, w_lo, preferred_element_type=f32)
    o_ref[...] = acc + b_ref[...]


def _adaln(c, w_ada, b_ada):
    bsz, d = c.shape
    n = w_ada.shape[1]
    bn = ADALN_BLOCK
    assert n % bn == 0
    return pl.pallas_call(
        _adaln_kernel,
        grid=(n // bn,),
        in_specs=[
            pl.BlockSpec((bsz, d), lambda j: (0, 0)),
            pl.BlockSpec((d, bn), lambda j: (0, j)),
            pl.BlockSpec((1, bn), lambda j: (0, j)),
        ],
        out_specs=pl.BlockSpec((bsz, bn), lambda j: (0, j)),
        out_shape=jax.ShapeDtypeStruct((bsz, n), f32),
        compiler_params=pltpu.CompilerParams(dimension_semantics=("arbitrary",)),
        name="adaln",
    )(c, w_ada, b_ada.reshape(1, n))


def _mixer_kernel(x_ref, g_ref, sc_ref, sh_ref, gt_ref, w_in_ref, cw_ref, cb_ref, lng_ref,
                  lnb_ref, pw_ref, ps_ref, w_out_ref, o_ref,
                  proj_ref, ubuf, pbuf, zbuf, ybuf, *, n_conv, conv_k):
    ts = x_ref.shape[1]
    n_slab = n_conv // LANES
    n_grp = len(POOL_WINDOWS)
    i = pl.program_id(1)

    @pl.when(i == 0)
    def _():
        ubuf[:, 0:CONV_HALO, :] = jnp.zeros((n_slab, CONV_HALO, LANES), f32)
        pbuf[:, 0:POOL_HALO, :] = jnp.zeros((n_grp, POOL_HALO, LANES), f32)

    x = x_ref[0]
    ms = jnp.mean(x * x, axis=-1, keepdims=True)
    h = (x * lax.rsqrt(ms + EPS)) * (g_ref[...] * (1.0 + sc_ref[0])) + sh_ref[0]
    proj_ref[...] = jnp.dot(h.astype(bf16), w_in_ref[...], preferred_element_type=f32)

    for rb in range(0, ts, ROW_BLOCK):
        rows = pl.ds(rb, ROW_BLOCK)
        for j in range(n_slab):
            a_val = proj_ref[rows, j * LANES:(j + 1) * LANES]
            a_gate = proj_ref[rows, n_conv + j * LANES:n_conv + (j + 1) * LANES]
            ubuf[j, pl.ds(CONV_HALO + rb, ROW_BLOCK), :] = a_val * jax.nn.sigmoid(a_gate)
        for j in range(n_grp):
            pbuf[j, pl.ds(POOL_HALO + rb, ROW_BLOCK), :] = (
                proj_ref[rows, 2 * n_conv + j * LANES:2 * n_conv + (j + 1) * LANES])

    t0 = (i * ts).astype(f32)
    for rb in range(0, ts, ROW_BLOCK):
        conv = []
        for j in range(n_slab):
            lanes = slice(j * LANES, (j + 1) * LANES)
            acc = jnp.broadcast_to(cb_ref[:, lanes], (ROW_BLOCK, LANES))
            for k in range(conv_k):
                off = CONV_HALO + rb - (conv_k - 1) + k
                acc = acc + cw_ref[k:k + 1, lanes] * ubuf[j, pl.ds(off, ROW_BLOCK), :]
            conv.append(acc)
        tot = conv[0]
        for j in range(1, n_slab):
            tot = tot + conv[j]
        mu = jnp.sum(tot, axis=-1, keepdims=True) * (1.0 / n_conv)
        cen = [cj - mu for cj in conv]
        sq = cen[0] * cen[0]
        for j in range(1, n_slab):
            sq = sq + cen[j] * cen[j]
        var = jnp.sum(sq, axis=-1, keepdims=True) * (1.0 / n_conv)
        rstd = lax.rsqrt(var + EPS)
        for j in range(n_slab):
            lanes = slice(j * LANES, (j + 1) * LANES)
            yn = cen[j] * rstd * lng_ref[:, lanes] + lnb_ref[:, lanes]
            ybuf[pl.ds(rb, ROW_BLOCK), lanes] = (yn * jax.nn.sigmoid(yn)).astype(bf16)

        t = t0 + (rb + lax.broadcasted_iota(jnp.int32, (ROW_BLOCK, LANES), 0)).astype(f32)
        for j, win in enumerate(POOL_WINDOWS):
            cur = pbuf[j, pl.ds(POOL_HALO + rb, ROW_BLOCK), :]
            s = cur
            for k in range(1, win):
                s = s + pbuf[j, pl.ds(POOL_HALO + rb - k, ROW_BLOCK), :]
            cnt = jnp.minimum(t + 1.0, float(win))
            zbuf[pl.ds(rb, ROW_BLOCK), j * LANES:(j + 1) * LANES] = (s / cnt - cur).astype(bf16)

    ubuf[:, 0:CONV_HALO, :] = ubuf[:, ts:ts + CONV_HALO, :]
    pbuf[:, 0:POOL_HALO, :] = pbuf[:, ts:ts + POOL_HALO, :]

    for q in range(pw_ref.shape[0]):
        cols = slice(q * MXU_DIM, (q + 1) * MXU_DIM)
        yb = jnp.dot(zbuf[:, cols], pw_ref[q], preferred_element_type=f32) * ps_ref[:, cols]
        ybuf[:, n_conv + q * MXU_DIM:n_conv + (q + 1) * MXU_DIM] = yb.astype(bf16)

    y = jnp.dot(ybuf[...], w_out_ref[...], preferred_element_type=f32)
    o_ref[0] = x_ref[0] + gt_ref[0] * y


def _resident(shape):
    zeros = (0,) * len(shape)
    return pl.BlockSpec(shape, lambda b, i: zeros, pipeline_mode=pl.Buffered(1))


def _per_batch(d):
    return pl.BlockSpec((1, 1, d), lambda b, i: (b, 0, 0))


def _mixer(x, g_mix, sc1, sh1, gt1, w_in, conv_w, conv_b, ln_g, ln_b, pool_bd, pool_scale, w_out):
    bsz, seq, d = x.shape
    conv_k, n_conv = conv_w.shape
    n_pool = pool_scale.shape[0]
    ts = MIXER_TILE
    assert seq % ts == 0 and ts % ROW_BLOCK == 0 and ROW_BLOCK % BF16_ROWS == 0
    assert n_conv % LANES == 0 and n_pool == len(POOL_WINDOWS) * LANES
    assert CONV_HALO >= conv_k - 1 and POOL_HALO >= max(POOL_WINDOWS) - 1
    kern = functools.partial(_mixer_kernel, n_conv=n_conv, conv_k=conv_k)
    tile = pl.BlockSpec((1, ts, d), lambda b, i: (b, i, 0))
    return pl.pallas_call(
        kern,
        grid=(bsz, seq // ts),
        in_specs=[
            tile, _resident((1, d)), _per_batch(d), _per_batch(d), _per_batch(d),
            _resident(w_in.shape), _resident(conv_w.shape), _resident((1, n_conv)),
            _resident((1, n_conv)), _resident((1, n_conv)), _resident(pool_bd.shape),
            _resident((1, n_pool)), _resident(w_out.shape),
        ],
        out_specs=tile,
        out_shape=jax.ShapeDtypeStruct(x.shape, f32),
        scratch_shapes=[
            pltpu.VMEM((ts, w_in.shape[1]), f32),
            pltpu.VMEM((n_conv // LANES, CONV_HALO + ts, LANES), f32),
            pltpu.VMEM((len(POOL_WINDOWS), POOL_HALO + ts, LANES), f32),
            pltpu.VMEM((ts, n_pool), bf16),
            pltpu.VMEM((ts, n_conv + n_pool), bf16),
        ],
        compiler_params=pltpu.CompilerParams(
            dimension_semantics=("arbitrary", "arbitrary"),
            vmem_limit_bytes=VMEM_LIMIT_BYTES),
        name="mixer",
    )(x, g_mix.reshape(1, d), sc1, sh1, gt1, w_in, conv_w, conv_b.reshape(1, n_conv),
      ln_g.reshape(1, n_conv), ln_b.reshape(1, n_conv), pool_bd, pool_scale.reshape(1, n_pool), w_out)


def _ffn_kernel(x_ref, g_ref, sc_ref, sh_ref, gt_ref, w_up_ref, fw_ref, fb_ref, w_down_ref,
                gf_ref, o_ref, hbuf, upbuf, carry, *abufs, d_ff):
    ts = x_ref.shape[1]
    n_chunk = d_ff // FFN_CHUNK
    part_chunks = pl.cdiv(n_chunk, len(abufs))
    slabs = FFN_CHUNK // LANES
    i = pl.program_id(1)

    @pl.when(i == 0)
    def _():
        carry[...] = jnp.zeros(carry.shape, f32)

    a_wr = [a.at[jnp.minimum(i, 0)] for a in abufs]
    a_rd = [a.at[jnp.minimum(pl.program_id(0), 0)] for a in abufs]

    x = x_ref[0]
    ms = jnp.mean(x * x, axis=-1, keepdims=True)
    h = (x * lax.rsqrt(ms + EPS)) * (g_ref[...] * (1.0 + sc_ref[0])) + sh_ref[0]
    hbuf[...] = h.astype(bf16)

    def conv3(buf, slab, col, rb):
        lanes = slice(col, col + LANES)
        acc = jnp.broadcast_to(fb_ref[:, lanes], (FFN_ROW_BLOCK, LANES))
        for k in range(3):
            acc = acc + fw_ref[k:k + 1, lanes] * buf[slab, pl.ds(FFN_HALO + rb - 2 + k, FFN_ROW_BLOCK), :]
        return acc

    for c in range(n_chunk):
        buf = upbuf.at[c % 2]
        cols = [c * FFN_CHUNK + s * LANES for s in range(slabs)]
        cols = cols + [d_ff + col for col in cols]
        for half in range(2):
            c0 = half * d_ff + c * FFN_CHUNK
            up = jnp.dot(hbuf[...], w_up_ref[:, c0:c0 + FFN_CHUNK], preferred_element_type=f32)
            for s in range(slabs):
                slab = half * slabs + s
                buf[slab, 0:FFN_HALO, :] = carry[c * 2 * slabs + slab]
                buf[slab, FFN_HALO:FFN_HALO + ts, :] = up[:, s * LANES:(s + 1) * LANES]
                carry[c * 2 * slabs + slab] = buf[slab, ts:ts + FFN_HALO, :]
        for rb in range(0, ts, FFN_ROW_BLOCK):
            for s in range(slabs):
                v = conv3(buf, s, cols[s], rb)
                g = conv3(buf, slabs + s, cols[slabs + s], rb)
                act = (g * jax.nn.sigmoid(g)) * v
                col = (c % part_chunks) * FFN_CHUNK + s * LANES
                a_wr[c // part_chunks][pl.ds(rb, FFN_ROW_BLOCK), col:col + LANES] = act.astype(bf16)

    y = None
    for p, part in enumerate(a_rd):
        k0 = p * part_chunks * FFN_CHUNK
        yp = jnp.dot(part[...], w_down_ref[k0:k0 + part.shape[1], :], preferred_element_type=f32)
        y = yp if y is None else y + yp
    x2 = x_ref[0] + gt_ref[0] * y
    ms2 = jnp.mean(x2 * x2, axis=-1, keepdims=True)
    o_ref[0] = (x2 * lax.rsqrt(ms2 + EPS)) * gf_ref[...]


def _split_even(items, n_parts):
    size = -(-len(items) // n_parts)
    return [items[k:k + size] for k in range(0, len(items), size)]


def _ffn(x, g_ffn, sc2, sh2, gt2, w_up, ffn_conv_w, ffn_conv_b, w_down, g_final):
    bsz, seq, d = x.shape
    d_ff = w_down.shape[0]
    ts = FFN_TILE
    assert seq % ts == 0 and ts % FFN_ROW_BLOCK == 0 and FFN_ROW_BLOCK % BF16_ROWS == 0
    assert d_ff % FFN_CHUNK == 0 and FFN_HALO >= ffn_conv_w.shape[0] - 1
    kern = functools.partial(_ffn_kernel, d_ff=d_ff)
    tile = pl.BlockSpec((1, ts, d), lambda b, i: (b, i, 0))
    n_slab = 2 * d_ff // LANES
    return pl.pallas_call(
        kern,
        grid=(bsz, seq // ts),
        in_specs=[
            tile, _resident((1, d)), _per_batch(d), _per_batch(d), _per_batch(d),
            _resident(w_up.shape), _resident(ffn_conv_w.shape), _resident((1, 2 * d_ff)),
            _resident(w_down.shape), _resident((1, d)),
        ],
        out_specs=tile,
        out_shape=jax.ShapeDtypeStruct(x.shape, f32),
        scratch_shapes=[
            pltpu.VMEM((ts, d), bf16),
            pltpu.VMEM((2, 2 * FFN_CHUNK // LANES, FFN_HALO + ts, LANES), f32),
            pltpu.VMEM((n_slab, FFN_HALO, LANES), f32),
        ] + [
            pltpu.VMEM((1, ts, FFN_CHUNK * len(chunks)), bf16)
            for chunks in _split_even(range(d_ff // FFN_CHUNK), ACT_PARTS)
        ],
        compiler_params=pltpu.CompilerParams(
            dimension_semantics=("arbitrary", "arbitrary"),
            vmem_limit_bytes=VMEM_LIMIT_BYTES),
        name="convffn",
    )(x, g_ffn.reshape(1, d), sc2, sh2, gt2, w_up, ffn_conv_w, ffn_conv_b.reshape(1, 2 * d_ff),
      w_down, g_final.reshape(1, d))


def kernel(x, c, w_ada, b_ada, g_mix, w_in, conv_w, conv_b, ln_g, ln_b, pool_w, pool_scale, w_out,
           g_ffn, w_up, ffn_conv_w, ffn_conv_b, w_down, g_final):
    bsz, seq, d = x.shape
    mod = _adaln(c, w_ada, b_ada)
    sh1, sc1, gt1, sh2, sc2, gt2 = [m.reshape(bsz, 1, d) for m in jnp.split(mod, 6, axis=-1)]

    n_grp, grp, _ = pool_w.shape
    per_tile = MXU_DIM // grp
    pool_bd = jnp.zeros((n_grp // per_tile, MXU_DIM, MXU_DIM), pool_w.dtype)
    for gi in range(n_grp):
        q, r = divmod(gi, per_tile)
        pool_bd = pool_bd.at[q, r * grp:(r + 1) * grp, r * grp:(r + 1) * grp].set(pool_w[gi])

    x1 = _mixer(x, g_mix, sc1, sh1, gt1, w_in.astype(bf16), conv_w, conv_b, ln_g, ln_b,
                pool_bd.astype(bf16), pool_scale, w_out.astype(bf16))
    return _ffn(x1, g_ffn, sc2, sh2, gt2, w_up.astype(bf16), ffn_conv_w, ffn_conv_b,
                w_down.astype(bf16), g_final)
```

```python
import functools

import jax
import jax.numpy as jnp
from jax import lax
from jax.experimental import pallas as pl
from jax.experimental.pallas import tpu as pltpu

EPS = 1e-6
POOL_WINDOWS = (2, 4, 8, 16)

LANES = 128
SUBLANES = 8
BF16_ROWS = 16
MXU_DIM = 256
VMEM_LIMIT_BYTES = 58 * 1024 * 1024

ADALN_BLOCK = 1024
MIXER_TILE = 1024
FFN_TILE = 1024
ROW_BLOCK = 32
FFN_ROW_BLOCK = 64
CONV_HALO = 32
POOL_HALO = 16
FFN_HALO = SUBLANES
FFN_CHUNK = MXU_DIM
ACT_PARTS = 11

f32 = jnp.float32
bf16 = jnp.bfloat16


def _split_bf16(a):
    hi = a.astype(bf16)
    lo = (a - hi.astype(f32)).astype(bf16)
    return hi, lo


def _adaln_kernel(c_ref, w_ref, b_ref, o_ref):
    c = c_ref[...]
    s = c * jax.nn.sigmoid(c)
    s_hi, s_lo = _split_bf16(s)
    w_hi, w_lo = _split_bf16(w_ref[...])
    acc = jnp.dot(s_hi, w_hi, preferred_element_type=f32)
    acc += jnp.dot(s_lo, w_hi, preferred_element_type=f32)
    acc += jnp.dot(s_hi, w_lo, preferred_element_type=f32)
    o_ref[...] = acc + b_ref[...]


def _adaln(c, w_ada, b_ada):
    bsz, d = c.shape
    n = w_ada.shape[1]
    bn = ADALN_BLOCK
    assert n % bn == 0
    return pl.pallas_call(
        _adaln_kernel,
        grid=(n // bn,),
        in_specs=[
            pl.BlockSpec((bsz, d), lambda j: (0, 0)),
            pl.BlockSpec((d, bn), lambda j: (0, j)),
            pl.BlockSpec((1, bn), lambda j: (0, j)),
        ],
        out_specs=pl.BlockSpec((bsz, bn), lambda j: (0, j)),
        out_shape=jax.ShapeDtypeStruct((bsz, n), f32),
        compiler_params=pltpu.CompilerParams(dimension_semantics=("arbitrary",)),
        name="adaln",
    )(c, w_ada, b_ada.reshape(1, n))


def _mixer_kernel(x_ref, g_ref, sc_ref, sh_ref, gt_ref, w_in_ref, cw_ref, cb_ref, lng_ref,
                  lnb_ref, pw_ref, ps_ref, w_out_ref, o_ref,
                  hmix, ubuf, cbuf, pbuf, zbuf, ybuf, *, n_conv, conv_k):
    ts = x_ref.shape[1]
    n_slab = n_conv // LANES
    n_grp = len(POOL_WINDOWS)
    i = pl.program_id(1)

    @pl.when(i == 0)
    def _():
        ubuf[:, 0:CONV_HALO, :] = jnp.zeros((n_slab, CONV_HALO, LANES), f32)
        pbuf[:, 0:POOL_HALO, :] = jnp.zeros((n_grp, POOL_HALO, LANES), f32)

    x = x_ref[0]
    ms = jnp.mean(x * x, axis=-1, keepdims=True)
    h = (x * lax.rsqrt(ms + EPS)) * (g_ref[...] * (1.0 + sc_ref[0])) + sh_ref[0]
    hmix[...] = h.astype(bf16)

    pair = MXU_DIM // LANES
    for q in range(n_slab // pair):
        c0 = q * MXU_DIM
        val = jnp.dot(hmix[...], w_in_ref[:, c0:c0 + MXU_DIM], preferred_element_type=f32)
        gate = jnp.dot(hmix[...], w_in_ref[:, n_conv + c0:n_conv + c0 + MXU_DIM],
                       preferred_element_type=f32)
        u = val * jax.nn.sigmoid(gate)
        for jj in range(pair):
            ubuf[q * pair + jj, CONV_HALO:CONV_HALO + ts, :] = u[:, jj * LANES:(jj + 1) * LANES]
        for j in range(q * pair, (q + 1) * pair):
            lanes = slice(j * LANES, (j + 1) * LANES)
            for rb in range(0, ts, ROW_BLOCK):
                acc = jnp.broadcast_to(cb_ref[:, lanes], (ROW_BLOCK, LANES))
                for k in range(conv_k):
                    off = CONV_HALO + rb - (conv_k - 1) + k
                    acc = acc + cw_ref[k:k + 1, lanes] * ubuf[j, pl.ds(off, ROW_BLOCK), :]
                cbuf[j, pl.ds(rb, ROW_BLOCK), :] = acc
    p = jnp.dot(hmix[...], w_in_ref[:, 2 * n_conv:], preferred_element_type=f32)
    for j in range(n_grp):
        pbuf[j, POOL_HALO:POOL_HALO + ts, :] = p[:, j * LANES:(j + 1) * LANES]

    t0 = (i * ts).astype(f32)
    for rb in range(0, ts, ROW_BLOCK):
        conv = [cbuf[j, pl.ds(rb, ROW_BLOCK), :] for j in range(n_slab)]
        tot = conv[0]
        for j in range(1, n_slab):
            tot = tot + conv[j]
        mu = jnp.sum(tot, axis=-1, keepdims=True) * (1.0 / n_conv)
        cen = [cj - mu for cj in conv]
        sq = cen[0] * cen[0]
        for j in range(1, n_slab):
            sq = sq + cen[j] * cen[j]
        var = jnp.sum(sq, axis=-1, keepdims=True) * (1.0 / n_conv)
        rstd = lax.rsqrt(var + EPS)
        for j in range(n_slab):
            lanes = slice(j * LANES, (j + 1) * LANES)
            yn = cen[j] * rstd * lng_ref[:, lanes] + lnb_ref[:, lanes]
            ybuf[pl.ds(rb, ROW_BLOCK), lanes] = (yn * jax.nn.sigmoid(yn)).astype(bf16)

        t = t0 + (rb + lax.broadcasted_iota(jnp.int32, (ROW_BLOCK, LANES), 0)).astype(f32)
        for j, win in enumerate(POOL_WINDOWS):
            cur = pbuf[j, pl.ds(POOL_HALO + rb, ROW_BLOCK), :]
            s = cur
            for k in range(1, win):
                s = s + pbuf[j, pl.ds(POOL_HALO + rb - k, ROW_BLOCK), :]
            cnt = jnp.minimum(t + 1.0, float(win))
            zbuf[pl.ds(rb, ROW_BLOCK), j * LANES:(j + 1) * LANES] = (s / cnt - cur).astype(bf16)

    for q in range(pw_ref.shape[0]):
        cols = slice(q * MXU_DIM, (q + 1) * MXU_DIM)
        yb = jnp.dot(zbuf[:, cols], pw_ref[q], preferred_element_type=f32) * ps_ref[:, cols]
        ybuf[:, n_conv + q * MXU_DIM:n_conv + (q + 1) * MXU_DIM] = yb.astype(bf16)

    y = jnp.dot(ybuf[...], w_out_ref[...], preferred_element_type=f32)
    o_ref[0] = x_ref[0] + gt_ref[0] * y


    ubuf[:, 0:CONV_HALO, :] = ubuf[:, ts:ts + CONV_HALO, :]
    pbuf[:, 0:POOL_HALO, :] = pbuf[:, ts:ts + POOL_HALO, :]


def _resident(shape):
    zeros = (0,) * len(shape)
    return pl.BlockSpec(shape, lambda b, i: zeros, pipeline_mode=pl.Buffered(1))


def _per_batch(d):
    return pl.BlockSpec((1, 1, d), lambda b, i: (b, 0, 0))


def _mixer(x, g_mix, sc1, sh1, gt1, w_in, conv_w, conv_b, ln_g, ln_b, pool_bd, pool_scale, w_out):
    bsz, seq, d = x.shape
    conv_k, n_conv = conv_w.shape
    n_pool = pool_scale.shape[0]
    ts = MIXER_TILE
    assert seq % ts == 0 and ts % ROW_BLOCK == 0 and ROW_BLOCK % BF16_ROWS == 0
    assert n_conv % MXU_DIM == 0 and n_pool == len(POOL_WINDOWS) * LANES
    assert CONV_HALO >= conv_k - 1 and POOL_HALO >= max(POOL_WINDOWS) - 1
    kern = functools.partial(_mixer_kernel, n_conv=n_conv, conv_k=conv_k)
    tile = pl.BlockSpec((1, ts, d), lambda b, i: (b, i, 0))
    return pl.pallas_call(
        kern,
        grid=(bsz, seq // ts),
        in_specs=[
            tile, _resident((1, d)), _per_batch(d), _per_batch(d), _per_batch(d),
            _resident(w_in.shape), _resident(conv_w.shape), _resident((1, n_conv)),
            _resident((1, n_conv)), _resident((1, n_conv)), _resident(pool_bd.shape),
            _resident((1, n_pool)), _resident(w_out.shape),
        ],
        out_specs=tile,
        out_shape=jax.ShapeDtypeStruct(x.shape, f32),
        scratch_shapes=[
            pltpu.VMEM((ts, d), bf16),
            pltpu.VMEM((n_conv // LANES, CONV_HALO + ts, LANES), f32),
            pltpu.VMEM((n_conv // LANES, ts, LANES), f32),
            pltpu.VMEM((len(POOL_WINDOWS), POOL_HALO + ts, LANES), f32),
            pltpu.VMEM((ts, n_pool), bf16),
            pltpu.VMEM((ts, n_conv + n_pool), bf16),
        ],
        compiler_params=pltpu.CompilerParams(
            dimension_semantics=("arbitrary", "arbitrary"),
            vmem_limit_bytes=VMEM_LIMIT_BYTES),
        name="mixer",
    )(x, g_mix.reshape(1, d), sc1, sh1, gt1, w_in, conv_w, conv_b.reshape(1, n_conv),
      ln_g.reshape(1, n_conv), ln_b.reshape(1, n_conv), pool_bd, pool_scale.reshape(1, n_pool), w_out)


def _ffn_kernel(x_ref, g_ref, sc_ref, sh_ref, gt_ref, w_up_ref, fw_ref, fb_ref, w_down_ref,
                gf_ref, o_ref, hbuf, upbuf, carry, *abufs, d_ff):
    ts = x_ref.shape[1]
    n_chunk = d_ff // FFN_CHUNK
    part_chunks = pl.cdiv(n_chunk, len(abufs))
    slabs = FFN_CHUNK // LANES
    i = pl.program_id(1)

    @pl.when(i == 0)
    def _():
        carry[...] = jnp.zeros(carry.shape, f32)

    a_wr = [a.at[jnp.minimum(i, 0)] for a in abufs]
    a_rd = [a.at[jnp.minimum(pl.program_id(0), 0)] for a in abufs]

    x = x_ref[0]
    ms = jnp.mean(x * x, axis=-1, keepdims=True)
    h = (x * lax.rsqrt(ms + EPS)) * (g_ref[...] * (1.0 + sc_ref[0])) + sh_ref[0]
    hbuf[...] = h.astype(bf16)

    def conv3(buf, slab, col, rb):
        lanes = slice(col, col + LANES)
        acc = jnp.broadcast_to(fb_ref[:, lanes], (FFN_ROW_BLOCK, LANES))
        for k in range(3):
            acc = acc + fw_ref[k:k + 1, lanes] * buf[slab, pl.ds(FFN_HALO + rb - 2 + k, FFN_ROW_BLOCK), :]
        return acc

    for c in range(n_chunk):
        buf = upbuf.at[c % 2]
        cols = [c * FFN_CHUNK + s * LANES for s in range(slabs)]
        cols = cols + [d_ff + col for col in cols]
        for half in range(2):
            c0 = half * d_ff + c * FFN_CHUNK
            up = jnp.dot(hbuf[...], w_up_ref[:, c0:c0 + FFN_CHUNK], preferred_element_type=f32)
            for s in range(slabs):
                slab = half * slabs + s
                buf[slab, 0:FFN_HALO, :] = carry[c * 2 * slabs + slab]
                buf[slab, FFN_HALO:FFN_HALO + ts, :] = up[:, s * LANES:(s + 1) * LANES]
                carry[c * 2 * slabs + slab] = buf[slab, ts:ts + FFN_HALO, :]
        for rb in range(0, ts, FFN_ROW_BLOCK):
            for s in range(slabs):
                v = conv3(buf, s, cols[s], rb)
                g = conv3(buf, slabs + s, cols[slabs + s], rb)
                act = (g * jax.nn.sigmoid(g)) * v
                col = (c % part_chunks) * FFN_CHUNK + s * LANES
                a_wr[c // part_chunks][pl.ds(rb, FFN_ROW_BLOCK), col:col + LANES] = act.astype(bf16)

    y = None
    for p, part in enumerate(a_rd):
        k0 = p * part_chunks * FFN_CHUNK
        yp = jnp.dot(part[...], w_down_ref[k0:k0 + part.shape[1], :], preferred_element_type=f32)
        y = yp if y is None else y + yp
    x2 = x_ref[0] + gt_ref[0] * y
    ms2 = jnp.mean(x2 * x2, axis=-1, keepdims=True)
    o_ref[0] = (x2 * lax.rsqrt(ms2 + EPS)) * gf_ref[...]


def _split_even(items, n_parts):
    size = -(-len(items) // n_parts)
    return [items[k:k + size] for k in range(0, len(items), size)]


def _ffn(x, g_ffn, sc2, sh2, gt2, w_up, ffn_conv_w, ffn_conv_b, w_down, g_final):
    bsz, seq, d = x.shape
    d_ff = w_down.shape[0]
    ts = FFN_TILE
    assert seq % ts == 0 and ts % FFN_ROW_BLOCK == 0 and FFN_ROW_BLOCK % BF16_ROWS == 0
    assert d_ff % FFN_CHUNK == 0 and FFN_HALO >= ffn_conv_w.shape[0] - 1
    kern = functools.partial(_ffn_kernel, d_ff=d_ff)
    tile = pl.BlockSpec((1, ts, d), lambda b, i: (b, i, 0))
    n_slab = 2 * d_ff // LANES
    return pl.pallas_call(
        kern,
        grid=(bsz, seq // ts),
        in_specs=[
            tile, _resident((1, d)), _per_batch(d), _per_batch(d), _per_batch(d),
            _resident(w_up.shape), _resident(ffn_conv_w.shape), _resident((1, 2 * d_ff)),
            _resident(w_down.shape), _resident((1, d)),
        ],
        out_specs=tile,
        out_shape=jax.ShapeDtypeStruct(x.shape, f32),
        scratch_shapes=[
            pltpu.VMEM((ts, d), bf16),
            pltpu.VMEM((2, 2 * FFN_CHUNK // LANES, FFN_HALO + ts, LANES), f32),
            pltpu.VMEM((n_slab, FFN_HALO, LANES), f32),
        ] + [
            pltpu.VMEM((1, ts, FFN_CHUNK * len(chunks)), bf16)
            for chunks in _split_even(range(d_ff // FFN_CHUNK), ACT_PARTS)
        ],
        compiler_params=pltpu.CompilerParams(
            dimension_semantics=("arbitrary", "arbitrary"),
            vmem_limit_bytes=VMEM_LIMIT_BYTES),
        name="convffn",
    )(x, g_ffn.reshape(1, d), sc2, sh2, gt2, w_up, ffn_conv_w, ffn_conv_b.reshape(1, 2 * d_ff),
      w_down, g_final.reshape(1, d))


def kernel(x, c, w_ada, b_ada, g_mix, w_in, conv_w, conv_b, ln_g, ln_b, pool_w, pool_scale, w_out,
           g_ffn, w_up, ffn_conv_w, ffn_conv_b, w_down, g_final):
    bsz, seq, d = x.shape
    mod = _adaln(c, w_ada, b_ada)
    sh1, sc1, gt1, sh2, sc2, gt2 = [m.reshape(bsz, 1, d) for m in jnp.split(mod, 6, axis=-1)]

    n_grp, grp, _ = pool_w.shape
    per_tile = MXU_DIM // grp
    pool_bd = jnp.zeros((n_grp // per_tile, MXU_DIM, MXU_DIM), pool_w.dtype)
    for gi in range(n_grp):
        q, r = divmod(gi, per_tile)
        pool_bd = pool_bd.at[q, r * grp:(r + 1) * grp, r * grp:(r + 1) * grp].set(pool_w[gi])

    x1 = _mixer(x, g_mix, sc1, sh1, gt1, w_in.astype(bf16), conv_w, conv_b, ln_g, ln_b,
                pool_bd.astype(bf16), pool_scale, w_out.astype(bf16))
    return _ffn(x1, g_ffn, sc2, sh2, gt2, w_up.astype(bf16), ffn_conv_w, ffn_conv_b,
                w_down.astype(bf16), g_final)
```

```python
import functools

import jax
import jax.numpy as jnp
from jax import lax
from jax.experimental import pallas as pl
from jax.experimental.pallas import tpu as pltpu

EPS = 1e-6
POOL_WINDOWS = (2, 4, 8, 16)

LANES = 128
SUBLANES = 8
BF16_ROWS = 16
MXU_DIM = 256
VMEM_LIMIT_BYTES = 58 * 1024 * 1024

ADALN_BLOCK = 1024
MIXER_TILE = 1024
CHUNK_ROW_PARTS = 2
FFN_TILE = 1024
ROW_BLOCK = 32
FFN_ROW_BLOCK = 64
CONV_HALO = 32
POOL_HALO = 16
FFN_HALO = SUBLANES
FFN_CHUNK = MXU_DIM
ACT_PARTS = 11

f32 = jnp.float32
bf16 = jnp.bfloat16


def _split_bf16(a):
    hi = a.astype(bf16)
    lo = (a - hi.astype(f32)).astype(bf16)
    return hi, lo


def _adaln_kernel(c_ref, w_ref, b_ref, o_ref):
    c = c_ref[...]
    s = c * jax.nn.sigmoid(c)
    s_hi, s_lo = _split_bf16(s)
    w_hi, w_lo = _split_bf16(w_ref[...])
    acc = jnp.dot(s_hi, w_hi, preferred_element_type=f32)
    acc += jnp.dot(s_lo, w_hi, preferred_element_type=f32)
    acc += jnp.dot(s_hi, w_lo, preferred_element_type=f32)
    o_ref[...] = acc + b_ref[...]


def _adaln(c, w_ada, b_ada):
    bsz, d = c.shape
    n = w_ada.shape[1]
    bn = ADALN_BLOCK
    assert n % bn == 0
    return pl.pallas_call(
        _adaln_kernel,
        grid=(n // bn,),
        in_specs=[
            pl.BlockSpec((bsz, d), lambda j: (0, 0)),
            pl.BlockSpec((d, bn), lambda j: (0, j)),
            pl.BlockSpec((1, bn), lambda j: (0, j)),
        ],
        out_specs=pl.BlockSpec((bsz, bn), lambda j: (0, j)),
        out_shape=jax.ShapeDtypeStruct((bsz, n), f32),
        compiler_params=pltpu.CompilerParams(dimension_semantics=("arbitrary",)),
        name="adaln",
    )(c, w_ada, b_ada.reshape(1, n))


def _mixer_kernel(x_ref, g_ref, sc_ref, sh_ref, gt_ref, w_in_ref, cw_ref, cb_ref, lng_ref,
                  lnb_ref, pw_ref, ps_ref, w_out_ref, o_ref,
                  hmix, ubuf, cbuf, pbuf, zbuf, ybuf, *, n_conv, conv_k):
    ts = x_ref.shape[1]
    n_slab = n_conv // LANES
    n_grp = len(POOL_WINDOWS)
    i = pl.program_id(1)

    @pl.when(i == 0)
    def _():
        ubuf[:, 0:CONV_HALO, :] = jnp.zeros((n_slab, CONV_HALO, LANES), f32)
        pbuf[:, 0:POOL_HALO, :] = jnp.zeros((n_grp, POOL_HALO, LANES), f32)

    x = x_ref[0]
    ms = jnp.mean(x * x, axis=-1, keepdims=True)
    h = (x * lax.rsqrt(ms + EPS)) * (g_ref[...] * (1.0 + sc_ref[0])) + sh_ref[0]
    hmix[...] = h.astype(bf16)

    pair = MXU_DIM // LANES
    part = ts // CHUNK_ROW_PARTS
    for q in range(n_slab // pair):
        c0 = q * MXU_DIM
        for r0 in range(0, ts, part):
            rows = pl.ds(r0, part)
            val = jnp.dot(hmix[rows, :], w_in_ref[:, c0:c0 + MXU_DIM], preferred_element_type=f32)
            gate = jnp.dot(hmix[rows, :], w_in_ref[:, n_conv + c0:n_conv + c0 + MXU_DIM],
                           preferred_element_type=f32)
            u = val * jax.nn.sigmoid(gate)
            for jj in range(pair):
                ubuf[q * pair + jj, pl.ds(CONV_HALO + r0, part), :] = u[:, jj * LANES:(jj + 1) * LANES]
            for j in range(q * pair, (q + 1) * pair):
                lanes = slice(j * LANES, (j + 1) * LANES)
                for rb in range(r0, r0 + part, ROW_BLOCK):
                    acc = jnp.broadcast_to(cb_ref[:, lanes], (ROW_BLOCK, LANES))
                    for k in range(conv_k):
                        off = CONV_HALO + rb - (conv_k - 1) + k
                        acc = acc + cw_ref[k:k + 1, lanes] * ubuf[j, pl.ds(off, ROW_BLOCK), :]
                    cbuf[j, pl.ds(rb, ROW_BLOCK), :] = acc
    p = jnp.dot(hmix[...], w_in_ref[:, 2 * n_conv:], preferred_element_type=f32)
    for j in range(n_grp):
        pbuf[j, POOL_HALO:POOL_HALO + ts, :] = p[:, j * LANES:(j + 1) * LANES]

    t0 = (i * ts).astype(f32)
    for rb in range(0, ts, ROW_BLOCK):
        conv = [cbuf[j, pl.ds(rb, ROW_BLOCK), :] for j in range(n_slab)]
        tot = conv[0]
        for j in range(1, n_slab):
            tot = tot + conv[j]
        mu = jnp.sum(tot, axis=-1, keepdims=True) * (1.0 / n_conv)
        cen = [cj - mu for cj in conv]
        sq = cen[0] * cen[0]
        for j in range(1, n_slab):
            sq = sq + cen[j] * cen[j]
        var = jnp.sum(sq, axis=-1, keepdims=True) * (1.0 / n_conv)
        rstd = lax.rsqrt(var + EPS)
        for j in range(n_slab):
            lanes = slice(j * LANES, (j + 1) * LANES)
            yn = cen[j] * rstd * lng_ref[:, lanes] + lnb_ref[:, lanes]
            ybuf[pl.ds(rb, ROW_BLOCK), lanes] = (yn * jax.nn.sigmoid(yn)).astype(bf16)

        t = t0 + (rb + lax.broadcasted_iota(jnp.int32, (ROW_BLOCK, LANES), 0)).astype(f32)
        for j, win in enumerate(POOL_WINDOWS):
            cur = pbuf[j, pl.ds(POOL_HALO + rb, ROW_BLOCK), :]
            s = cur
            for k in range(1, win):
                s = s + pbuf[j, pl.ds(POOL_HALO + rb - k, ROW_BLOCK), :]
            cnt = jnp.minimum(t + 1.0, float(win))
            zbuf[pl.ds(rb, ROW_BLOCK), j * LANES:(j + 1) * LANES] = (s / cnt - cur).astype(bf16)

    for q in range(pw_ref.shape[0]):
        cols = slice(q * MXU_DIM, (q + 1) * MXU_DIM)
        yb = jnp.dot(zbuf[:, cols], pw_ref[q], preferred_element_type=f32) * ps_ref[:, cols]
        ybuf[:, n_conv + q * MXU_DIM:n_conv + (q + 1) * MXU_DIM] = yb.astype(bf16)

    y = jnp.dot(ybuf[...], w_out_ref[...], preferred_element_type=f32)
    o_ref[0] = x_ref[0] + gt_ref[0] * y


    ubuf[:, 0:CONV_HALO, :] = ubuf[:, ts:ts + CONV_HALO, :]
    pbuf[:, 0:POOL_HALO, :] = pbuf[:, ts:ts + POOL_HALO, :]


def _resident(shape):
    zeros = (0,) * len(shape)
    return pl.BlockSpec(shape, lambda b, i: zeros, pipeline_mode=pl.Buffered(1))


def _per_batch(d):
    return pl.BlockSpec((1, 1, d), lambda b, i: (b, 0, 0))


def _mixer(x, g_mix, sc1, sh1, gt1, w_in, conv_w, conv_b, ln_g, ln_b, pool_bd, pool_scale, w_out):
    bsz, seq, d = x.shape
    conv_k, n_conv = conv_w.shape
    n_pool = pool_scale.shape[0]
    ts = MIXER_TILE
    assert seq % ts == 0 and ts % ROW_BLOCK == 0 and ROW_BLOCK % BF16_ROWS == 0
    assert n_conv % MXU_DIM == 0 and n_pool == len(POOL_WINDOWS) * LANES
    assert CONV_HALO >= conv_k - 1 and POOL_HALO >= max(POOL_WINDOWS) - 1
    kern = functools.partial(_mixer_kernel, n_conv=n_conv, conv_k=conv_k)
    tile = pl.BlockSpec((1, ts, d), lambda b, i: (b, i, 0))
    return pl.pallas_call(
        kern,
        grid=(bsz, seq // ts),
        in_specs=[
            tile, _resident((1, d)), _per_batch(d), _per_batch(d), _per_batch(d),
            _resident(w_in.shape), _resident(conv_w.shape), _resident((1, n_conv)),
            _resident((1, n_conv)), _resident((1, n_conv)), _resident(pool_bd.shape),
            _resident((1, n_pool)), _resident(w_out.shape),
        ],
        out_specs=tile,
        out_shape=jax.ShapeDtypeStruct(x.shape, f32),
        scratch_shapes=[
            pltpu.VMEM((ts, d), bf16),
            pltpu.VMEM((n_conv // LANES, CONV_HALO + ts, LANES), f32),
            pltpu.VMEM((n_conv // LANES, ts, LANES), f32),
            pltpu.VMEM((len(POOL_WINDOWS), POOL_HALO + ts, LANES), f32),
            pltpu.VMEM((ts, n_pool), bf16),
            pltpu.VMEM((ts, n_conv + n_pool), bf16),
        ],
        compiler_params=pltpu.CompilerParams(
            dimension_semantics=("arbitrary", "arbitrary"),
            vmem_limit_bytes=VMEM_LIMIT_BYTES),
        name="mixer",
    )(x, g_mix.reshape(1, d), sc1, sh1, gt1, w_in, conv_w, conv_b.reshape(1, n_conv),
      ln_g.reshape(1, n_conv), ln_b.reshape(1, n_conv), pool_bd, pool_scale.reshape(1, n_pool), w_out)


def _ffn_kernel(x_ref, g_ref, sc_ref, sh_ref, gt_ref, w_up_ref, fw_ref, fb_ref, w_down_ref,
                gf_ref, o_ref, hbuf, upbuf, carry, *abufs, d_ff):
    ts = x_ref.shape[1]
    n_chunk = d_ff // FFN_CHUNK
    part_chunks = pl.cdiv(n_chunk, len(abufs))
    slabs = FFN_CHUNK // LANES
    i = pl.program_id(1)

    @pl.when(i == 0)
    def _():
        carry[...] = jnp.zeros(carry.shape, f32)

    a_wr = [a.at[jnp.minimum(i, 0)] for a in abufs]
    a_rd = [a.at[jnp.minimum(pl.program_id(0), 0)] for a in abufs]

    x = x_ref[0]
    ms = jnp.mean(x * x, axis=-1, keepdims=True)
    h = (x * lax.rsqrt(ms + EPS)) * (g_ref[...] * (1.0 + sc_ref[0])) + sh_ref[0]
    hbuf[...] = h.astype(bf16)

    def conv3(buf, slab, col, rb):
        lanes = slice(col, col + LANES)
        acc = jnp.broadcast_to(fb_ref[:, lanes], (FFN_ROW_BLOCK, LANES))
        for k in range(3):
            acc = acc + fw_ref[k:k + 1, lanes] * buf[slab, pl.ds(FFN_HALO + rb - 2 + k, FFN_ROW_BLOCK), :]
        return acc

    for c in range(n_chunk):
        buf = upbuf.at[c % 2]
        cols = [c * FFN_CHUNK + s * LANES for s in range(slabs)]
        cols = cols + [d_ff + col for col in cols]
        for half in range(2):
            c0 = half * d_ff + c * FFN_CHUNK
            up = jnp.dot(hbuf[...], w_up_ref[:, c0:c0 + FFN_CHUNK], preferred_element_type=f32)
            for s in range(slabs):
                slab = half * slabs + s
                buf[slab, 0:FFN_HALO, :] = carry[c * 2 * slabs + slab]
                buf[slab, FFN_HALO:FFN_HALO + ts, :] = up[:, s * LANES:(s + 1) * LANES]
                carry[c * 2 * slabs + slab] = buf[slab, ts:ts + FFN_HALO, :]
        for rb in range(0, ts, FFN_ROW_BLOCK):
            for s in range(slabs):
                v = conv3(buf, s, cols[s], rb)
                g = conv3(buf, slabs + s, cols[slabs + s], rb)
                act = (g * jax.nn.sigmoid(g)) * v
                col = (c % part_chunks) * FFN_CHUNK + s * LANES
                a_wr[c // part_chunks][pl.ds(rb, FFN_ROW_BLOCK), col:col + LANES] = act.astype(bf16)

    y = None
    for p, part in enumerate(a_rd):
        k0 = p * part_chunks * FFN_CHUNK
        yp = jnp.dot(part[...], w_down_ref[k0:k0 + part.shape[1], :], preferred_element_type=f32)
        y = yp if y is None else y + yp
    x2 = x_ref[0] + gt_ref[0] * y
    ms2 = jnp.mean(x2 * x2, axis=-1, keepdims=True)
    o_ref[0] = (x2 * lax.rsqrt(ms2 + EPS)) * gf_ref[...]


def _split_even(items, n_parts):
    size = -(-len(items) // n_parts)
    return [items[k:k + size] for k in range(0, len(items), size)]


def _ffn(x, g_ffn, sc2, sh2, gt2, w_up, ffn_conv_w, ffn_conv_b, w_down, g_final):
    bsz, seq, d = x.shape
    d_ff = w_down.shape[0]
    ts = FFN_TILE
    assert seq % ts == 0 and ts % FFN_ROW_BLOCK == 0 and FFN_ROW_BLOCK % BF16_ROWS == 0
    assert d_ff % FFN_CHUNK == 0 and FFN_HALO >= ffn_conv_w.shape[0] - 1
    kern = functools.partial(_ffn_kernel, d_ff=d_ff)
    tile = pl.BlockSpec((1, ts, d), lambda b, i: (b, i, 0))
    n_slab = 2 * d_ff // LANES
    return pl.pallas_call(
        kern,
        grid=(bsz, seq // ts),
        in_specs=[
            tile, _resident((1, d)), _per_batch(d), _per_batch(d), _per_batch(d),
            _resident(w_up.shape), _resident(ffn_conv_w.shape), _resident((1, 2 * d_ff)),
            _resident(w_down.shape), _resident((1, d)),
        ],
        out_specs=tile,
        out_shape=jax.ShapeDtypeStruct(x.shape, f32),
        scratch_shapes=[
            pltpu.VMEM((ts, d), bf16),
            pltpu.VMEM((2, 2 * FFN_CHUNK // LANES, FFN_HALO + ts, LANES), f32),
            pltpu.VMEM((n_slab, FFN_HALO, LANES), f32),
        ] + [
            pltpu.VMEM((1, ts, FFN_CHUNK * len(chunks)), bf16)
            for chunks in _split_even(range(d_ff // FFN_CHUNK), ACT_PARTS)
        ],
        compiler_params=pltpu.CompilerParams(
            dimension_semantics=("arbitrary", "arbitrary"),
            vmem_limit_bytes=VMEM_LIMIT_BYTES),
        name="convffn",
    )(x, g_ffn.reshape(1, d), sc2, sh2, gt2, w_up, ffn_conv_w, ffn_conv_b.reshape(1, 2 * d_ff),
      w_down, g_final.reshape(1, d))


def kernel(x, c, w_ada, b_ada, g_mix, w_in, conv_w, conv_b, ln_g, ln_b, pool_w, pool_scale, w_out,
           g_ffn, w_up, ffn_conv_w, ffn_conv_b, w_down, g_final):
    bsz, seq, d = x.shape
    mod = _adaln(c, w_ada, b_ada)
    sh1, sc1, gt1, sh2, sc2, gt2 = [m.reshape(bsz, 1, d) for m in jnp.split(mod, 6, axis=-1)]

    n_grp, grp, _ = pool_w.shape
    per_tile = MXU_DIM // grp
    pool_bd = jnp.zeros((n_grp // per_tile, MXU_DIM, MXU_DIM), pool_w.dtype)
    for gi in range(n_grp):
        q, r = divmod(gi, per_tile)
        pool_bd = pool_bd.at[q, r * grp:(r + 1) * grp, r * grp:(r + 1) * grp].set(pool_w[gi])

    x1 = _mixer(x, g_mix, sc1, sh1, gt1, w_in.astype(bf16), conv_w, conv_b, ln_g, ln_b,
                pool_bd.astype(bf16), pool_scale, w_out.astype(bf16))
    return _ffn(x1, g_ffn, sc2, sh2, gt2, w_up.astype(bf16), ffn_conv_w, ffn_conv_b,
                w_down.astype(bf16), g_final)
```
